```python
import math
import jax
import jax.numpy as jnp
from jax import lax
import numpy as np

D_MODEL = 2048
BATCH = 32
SEQ = 256
DEPTH = 4
DEC_BATCH = 2
DEC_SEQ = 1024
PAST_LEN = 256

GRID_W = 64
MIX_W = D_MODEL // 2
D_RNN = MIX_W
RNN_HEADS = 16
RNN_BLK = D_RNN // RNN_HEADS
CONV_W = 4
CONV_PAD_LO = 1
CONV_PAD_HI = 2
RG_C = 8.0
ATTN_HEADS = 8
DV = MIX_W // ATTN_HEADS
DH = DV // 2
ROPE_AXIS_DIM = DH // 2
ROPE_THETA = 10000.0
Q_BLOCK = 128
D_GMLP = MIX_W
GMLP_GROUPS = 16
GMLP_CH = D_GMLP // GMLP_GROUPS
CHUNK = 128
N_BRANCH = 3
IN_WIDTH = 7 * MIX_W + N_BRANCH * D_MODEL
IN_SPLITS = (MIX_W, 2 * MIX_W, 3 * MIX_W, 4 * MIX_W, 5 * MIX_W, 6 * MIX_W, 7 * MIX_W)
D_FF = 5632
N_EXPERTS = 8
TOP_K = 2
D_FF_EXPERT = 7168
MOE_BLOCK = 128
N_DENSE = (DEPTH + 1) // 2
N_MOE = DEPTH // 2
EPS = 1e-6

kernel_name = "hybrid_diffusion_rglru_diffattn_gmlp_step"


def rmsnorm(x, g):
    xf = x.astype(jnp.float32)
    y = xf * lax.rsqrt(jnp.mean(xf * xf, axis=-1, keepdims=True) + EPS)
    return (y * g.astype(jnp.float32)).astype(x.dtype)


def modulate(x, g, shift, scale):
    return rmsnorm(x, g) * (1.0 + scale) + shift


def axial_rope_tables(n_tokens):
    n_rows = n_tokens // GRID_W
    row = jnp.repeat(jnp.arange(n_rows), GRID_W).astype(jnp.float32)
    col = jnp.tile(jnp.arange(GRID_W), n_rows).astype(jnp.float32)
    inv_freq = 1.0 / (ROPE_THETA ** (jnp.arange(0, ROPE_AXIS_DIM, 2, dtype=jnp.float32) / ROPE_AXIS_DIM))
    ang_r = row[:, None] * inv_freq[None, :]
    ang_c = col[:, None] * inv_freq[None, :]
    return (jnp.cos(ang_r), jnp.sin(ang_r), jnp.cos(ang_c), jnp.sin(ang_c))


def apply_axial_rope(x, tables):
    cos_r, sin_r, cos_c, sin_c = tables
    shape = (1, x.shape[1]) + (1,) * (x.ndim - 3) + (cos_r.shape[-1],)

    def rot(z, cos, sin):
        cos = cos.reshape(shape)
        sin = sin.reshape(shape)
        z1, z2 = jnp.split(z.astype(jnp.float32), 2, axis=-1)
        return jnp.concatenate([z1 * cos - z2 * sin, z2 * cos + z1 * sin], axis=-1)

    x_row, x_col = jnp.split(x, 2, axis=-1)
    return jnp.concatenate([rot(x_row, cos_r, sin_r), rot(x_col, cos_c, sin_c)], axis=-1).astype(x.dtype)


def depthwise_conv_centred(x, w, b):
    n = x.shape[1]
    xp = jnp.pad(x, ((0, 0), (CONV_PAD_LO, CONV_PAD_HI), (0, 0)))
    out = xp[:, 0:n] * w[0]
    for j in range(1, CONV_W):
        out = out + xp[:, j:j + n] * w[j]
    return out + b


def block_diag_linear(x, w, b):
    bsz, n, width = x.shape
    xh = x.reshape(bsz, n, RNN_HEADS, RNN_BLK)
    return jnp.einsum("bnhi,hij->bnhj", xh, w).reshape(bsz, n, width) + b


def rglru_scan(x, w_a, b_a, w_x, b_x, lam, h0, reverse):
    r = jax.nn.sigmoid(block_diag_linear(x, w_a, b_a).astype(jnp.float32))
    i = jax.nn.sigmoid(block_diag_linear(x, w_x, b_x).astype(jnp.float32))
    log_a = RG_C * r * jax.nn.log_sigmoid(lam.astype(jnp.float32))
    a = jnp.exp(log_a)
    b = jnp.sqrt(-jnp.expm1(2.0 * log_a)) * (i * x.astype(jnp.float32))

    def step(h, ab):
        a_t, b_t = ab
        h = a_t * h + b_t
        return h, h

    h_last, hs = lax.scan(step, h0, (jnp.swapaxes(a, 0, 1), jnp.swapaxes(b, 0, 1)), reverse=reverse)
    return jnp.swapaxes(hs, 0, 1), h_last


def diff_attention(q, k, v, lam):
    bsz, n_q = q.shape[0], q.shape[1]
    n_blk = n_q // Q_BLOCK
    q_blocks = jnp.moveaxis(q.reshape(bsz, n_blk, Q_BLOCK, ATTN_HEADS, 2, DH), 1, 0)
    scale = DH ** -0.5

    def one_block(q_blk):
        s = jnp.einsum("bqhmd,bkhmd->bhmqk", q_blk, k, preferred_element_type=jnp.float32) * scale
        p = jax.nn.softmax(s, axis=-1)
        w = p[:, :, 0] - lam * p[:, :, 1]
        return jnp.einsum("bhqk,bkhd->bqhd", w.astype(v.dtype), v)

    o = lax.map(one_block, q_blocks)
    return jnp.moveaxis(o, 0, 1).reshape(bsz, n_q, ATTN_HEADS, DV)


def chunk_gmlp(u, v, g_v, w_s, b_s):
    bsz, n, _ = v.shape
    v = rmsnorm(v, g_v).reshape(bsz, n // CHUNK, CHUNK, GMLP_GROUPS, GMLP_CH)
    s = jnp.einsum("gpq,bnqgc->bnpgc", w_s, v) + b_s.T[None, None, :, :, None]
    return u * s.reshape(bsz, n, D_GMLP)


def token_mixers(h, p, lam_init, rope, ctx_k, ctx_v, ctx_state):
    bsz, n, _ = h.shape
    latent = ctx_k is not None
    proj = jnp.dot(h, p["w_in"])
    x_r, g_r, q, k, v, u_c, v_c, gate_logits = jnp.split(proj, IN_SPLITS, axis=-1)

    x_r = depthwise_conv_centred(x_r, p["conv_w"], p["conv_b"])
    if latent:
        h0 = ctx_state.astype(jnp.float32)
    else:
        h0 = jnp.zeros((bsz, 2, D_RNN), jnp.float32)
    h_f, last_f = rglru_scan(x_r, p["rg_wa"][0], p["rg_ba"][0], p["rg_wx"][0], p["rg_bx"][0], p["rg_lambda"][0], h0[:, 0], False)
    h_b, last_b = rglru_scan(x_r, p["rg_wa"][1], p["rg_ba"][1], p["rg_wx"][1], p["rg_bx"][1], p["rg_lambda"][1], h0[:, 1], True)
    y_a = (h_f + h_b).astype(h.dtype) * jax.nn.gelu(g_r)

    q = rmsnorm(q.reshape(bsz, n, ATTN_HEADS, 2, DH), p["q_norm_g"])
    k = rmsnorm(k.reshape(bsz, n, ATTN_HEADS, 2, DH), p["k_norm_g"])
    v = v.reshape(bsz, n, ATTN_HEADS, DV)
    lq = p["lambda_q"].astype(jnp.float32)
    lk = p["lambda_k"].astype(jnp.float32)
    lam = jnp.exp(jnp.sum(lq[0] * lk[0])) - jnp.exp(jnp.sum(lq[1] * lk[1])) + lam_init
    if latent:
        q = apply_axial_rope(q, rope)
        k = apply_axial_rope(k, rope)
        keys = jnp.concatenate([ctx_k.reshape(bsz, -1, ATTN_HEADS, 2, DH).astype(k.dtype), k], axis=1)
        vals = jnp.concatenate([ctx_v.astype(v.dtype), v], axis=1)
    else:
        keys, vals = k, v
    o = diff_attention(q, keys, vals, lam)
    y_b = (rmsnorm(o, p["subln_g"]) * (1.0 - lam_init)).reshape(bsz, n, MIX_W)

    y_c = chunk_gmlp(jax.nn.gelu(u_c), jax.nn.gelu(v_c), p["gmlp_norm_g"], p["gmlp_ws"], p["gmlp_bs"])

    gates = jax.nn.sigmoid(gate_logits.reshape(bsz, n, N_BRANCH, D_MODEL))
    branches = jnp.stack([y_a, y_b, y_c], axis=2)
    branch_out = jnp.einsum("bnim,imd->bnid", branches, p["w_branch"])
    out = jnp.dot(jnp.sum(gates * branch_out, axis=2), p["w_o"])
    if latent:
        return out, None
    ctx_tensors = (k.reshape(bsz, n, ATTN_HEADS, 2 * DH), v, jnp.stack([last_f, last_b], axis=1).astype(h.dtype))
    return out, ctx_tensors


def swiglu(x, w1, w3, w2):
    return jnp.dot(jax.nn.silu(jnp.dot(x, w1)) * jnp.dot(x, w3), w2)


def moe_swiglu(x, router_w, w1, w3, w2):
    bsz, n, d = x.shape
    t = bsz * n
    xt = x.reshape(t, d)
    logits = jnp.dot(xt, router_w).astype(jnp.float32)
    top_logit, top_idx = lax.top_k(logits, TOP_K)
    gate = jax.nn.softmax(top_logit, axis=-1)
    slot_e = top_idx.reshape(-1)
    slot_tok = jnp.repeat(jnp.arange(t), TOP_K)
    slot_g = gate.reshape(-1)
    order = jnp.argsort(slot_e)
    e_s = slot_e[order]
    tok_s = slot_tok[order]
    g_s = slot_g[order]
    counts = jnp.zeros((N_EXPERTS,), jnp.int32).at[slot_e].add(1)
    padded = (counts + MOE_BLOCK - 1) // MOE_BLOCK * MOE_BLOCK
    start = jnp.cumsum(counts) - counts
    pend = jnp.cumsum(padded)
    pstart = pend - padded
    dest = pstart[e_s] + (jnp.arange(t * TOP_K) - start[e_s])
    n_blocks = -(-(t * TOP_K) // MOE_BLOCK) + N_EXPERTS
    xbuf = jnp.zeros((n_blocks * MOE_BLOCK, d), x.dtype).at[dest].set(xt[tok_s])
    block_e = jnp.minimum(jnp.searchsorted(pend, jnp.arange(n_blocks) * MOE_BLOCK, side="right"), N_EXPERTS - 1)

    def run_block(args):
        xb, e = args
        return swiglu(xb, w1[e], w3[e], w2[e])

    ybuf = lax.map(run_block, (xbuf.reshape(n_blocks, MOE_BLOCK, d), block_e))
    y_slot = ybuf.reshape(-1, d)[dest] * g_s[:, None].astype(x.dtype)
    y = jnp.zeros((t, d), x.dtype).at[tok_s].add(y_slot)
    return y.reshape(bsz, n, d)


def setup_inputs(seed: int = 0) -> dict:
    key = jax.random.key(seed)
    ks = jax.random.split(key, 40)
    f32 = jnp.float32
    L = DEPTH

    def nrm(k, shape, scale):
        return jax.random.normal(k, shape, f32) * scale

    a_target = jax.random.uniform(ks[19], (L, 2, D_RNN), f32, 0.9, 0.999)
    sig = a_target ** (1.0 / RG_C)
    return {
        "x_prompt": nrm(ks[0], (BATCH, SEQ, D_MODEL), 1.0),
        "x_sample": nrm(ks[1], (DEC_BATCH, DEC_SEQ, D_MODEL), 1.0),
        "cache_k": nrm(ks[2], (DEC_BATCH, L, PAST_LEN, ATTN_HEADS, 2 * DH), 1.0),
        "cache_v": nrm(ks[3], (DEC_BATCH, L, PAST_LEN, ATTN_HEADS, DV), 1.0),
        "state_rglru": nrm(ks[4], (DEC_BATCH, L, 2, D_RNN), 0.5),
        "c": nrm(ks[5], (DEC_BATCH, D_MODEL), 1.0),
        "c_ctx": nrm(ks[6], (D_MODEL,), 1.0),
        "ada_w": nrm(ks[7], (L, D_MODEL, 6 * D_MODEL), 0.5 * D_MODEL ** -0.5),
        "ada_b": nrm(ks[8], (L, 6 * D_MODEL), 0.02),
        "norm1_g": 1.0 + nrm(ks[9], (L, D_MODEL), 0.02),
        "norm2_g": 1.0 + nrm(ks[10], (L, D_MODEL), 0.02),
        "w_in": nrm(ks[11], (L, D_MODEL, IN_WIDTH), D_MODEL ** -0.5),
        "conv_w": nrm(ks[12], (L, CONV_W, D_RNN), CONV_W ** -0.5),
        "conv_b": nrm(ks[13], (L, D_RNN), 0.02),
        "rg_wa": nrm(ks[14], (L, 2, RNN_HEADS, RNN_BLK, RNN_BLK), RNN_BLK ** -0.5),
        "rg_ba": nrm(ks[15], (L, 2, D_RNN), 0.02),
        "rg_wx": nrm(ks[16], (L, 2, RNN_HEADS, RNN_BLK, RNN_BLK), RNN_BLK ** -0.5),
        "rg_bx": nrm(ks[17], (L, 2, D_RNN), 0.02),
        "rg_lambda": jnp.log(sig) - jnp.log1p(-sig),
        "q_norm_g": 1.0 + nrm(ks[20], (L, DH), 0.02),
        "k_norm_g": 1.0 + nrm(ks[21], (L, DH), 0.02),
        "lambda_q": nrm(ks[22], (L, 2, DH), 0.1),
        "lambda_k": nrm(ks[23], (L, 2, DH), 0.1),
        "subln_g": 1.0 + nrm(ks[24], (L, DV), 0.02),
        "gmlp_norm_g": 1.0 + nrm(ks[25], (L, D_GMLP), 0.02),
        "gmlp_ws": nrm(ks[26], (L, GMLP_GROUPS, CHUNK, CHUNK), CHUNK ** -0.5),
        "gmlp_bs": 1.0 + nrm(ks[27], (L, GMLP_GROUPS, CHUNK), 0.02),
        "w_branch": nrm(ks[28], (L, N_BRANCH, MIX_W, D_MODEL), MIX_W ** -0.5),
        "w_o": nrm(ks[29], (L, D_MODEL, D_MODEL), D_MODEL ** -0.5),
        "ffn_w1": nrm(ks[30], (N_DENSE, D_MODEL, D_FF), D_MODEL ** -0.5),
        "ffn_w3": nrm(ks[31], (N_DENSE, D_MODEL, D_FF), D_MODEL ** -0.5),
        "ffn_w2": nrm(ks[32], (N_DENSE, D_FF, D_MODEL), D_FF ** -0.5),
        "router_w": nrm(ks[33], (N_MOE, D_MODEL, N_EXPERTS), D_MODEL ** -0.5),
        "moe_w1": nrm(ks[34], (N_MOE, N_EXPERTS, D_MODEL, D_FF_EXPERT), D_MODEL ** -0.5),
        "moe_w3": nrm(ks[35], (N_MOE, N_EXPERTS, D_MODEL, D_FF_EXPERT), D_MODEL ** -0.5),
        "moe_w2": nrm(ks[36], (N_MOE, N_EXPERTS, D_FF_EXPERT, D_MODEL), D_FF_EXPERT ** -0.5),
    }


def reference(x_prompt, x_sample, cache_k, cache_v, state_rglru, c, c_ctx, ada_w, ada_b, norm1_g, norm2_g,
              w_in, conv_w, conv_b, rg_wa, rg_ba, rg_wx, rg_bx, rg_lambda, q_norm_g, k_norm_g, lambda_q, lambda_k,
              subln_g, gmlp_norm_g, gmlp_ws, gmlp_bs, w_branch, w_o, ffn_w1, ffn_w3, ffn_w2, router_w,
              moe_w1, moe_w3, moe_w2):
    rope = axial_rope_tables(x_sample.shape[1])
    xp = x_prompt
    xs = x_sample
    k_list, v_list, s_list = [], [], []
    for l in range(DEPTH):
        p = {
            "w_in": w_in[l], "conv_w": conv_w[l], "conv_b": conv_b[l],
            "rg_wa": rg_wa[l], "rg_ba": rg_ba[l], "rg_wx": rg_wx[l], "rg_bx": rg_bx[l], "rg_lambda": rg_lambda[l],
            "q_norm_g": q_norm_g[l], "k_norm_g": k_norm_g[l], "lambda_q": lambda_q[l], "lambda_k": lambda_k[l],
            "subln_g": subln_g[l], "gmlp_norm_g": gmlp_norm_g[l], "gmlp_ws": gmlp_ws[l], "gmlp_bs": gmlp_bs[l],
            "w_branch": w_branch[l], "w_o": w_o[l],
        }
        lam_init = 0.8 - 0.6 * math.exp(-0.3 * l)
        mod_p = (jnp.dot(jax.nn.silu(c_ctx), ada_w[l]) + ada_b[l])[None, None, :]
        mod_s = (jnp.dot(jax.nn.silu(c), ada_w[l]) + ada_b[l])[:, None, :]
        sh1p, sc1p, g1p, sh2p, sc2p, g2p = jnp.split(mod_p, 6, axis=-1)
        sh1s, sc1s, g1s, sh2s, sc2s, g2s = jnp.split(mod_s, 6, axis=-1)

        mix_p, (k_ctx, v_ctx, s_ctx) = token_mixers(modulate(xp, norm1_g[l], sh1p, sc1p), p, lam_init,
                                                    None, None, None, None)
        mix_s, _ = token_mixers(modulate(xs, norm1_g[l], sh1s, sc1s), p, lam_init,
                                rope, cache_k[:, l], cache_v[:, l], state_rglru[:, l])
        xp = xp + g1p * mix_p
        xs = xs + g1s * mix_s
        k_list.append(k_ctx)
        v_list.append(v_ctx)
        s_list.append(s_ctx)

        hp = modulate(xp, norm2_g[l], sh2p, sc2p)
        hs = modulate(xs, norm2_g[l], sh2s, sc2s)
        i = l // 2
        if l % 2 == 0:
            xp = xp + g2p * swiglu(hp, ffn_w1[i], ffn_w3[i], ffn_w2[i])
            xs = xs + g2s * swiglu(hs, ffn_w1[i], ffn_w3[i], ffn_w2[i])
        else:
            xp = xp + g2p * moe_swiglu(hp, router_w[i], moe_w1[i], moe_w3[i], moe_w2[i])
            xs = xs + g2s * moe_swiglu(hs, router_w[i], moe_w1[i], moe_w3[i], moe_w2[i])

    y_prompt = xp
    y_sample = xs
    new_cache_k = jnp.stack(k_list, axis=1)
    new_cache_v = jnp.stack(v_list, axis=1)
    new_state_rglru = jnp.stack(s_list, axis=1)
    return (y_prompt, y_sample, new_cache_k, new_cache_v, new_state_rglru)
```

```python
import functools
import math

import jax
import jax.numpy as jnp
from jax import lax
from jax.experimental import pallas as pl
from jax.experimental.pallas import tpu as pltpu

F32 = jnp.float32
BF16 = jnp.bfloat16

GRID_W = 64
CONV_W = 4
RG_C = 8.0
ROPE_THETA = 10000.0
CHUNK = 128
TOP_K = 2
EPS = 1e-6

LANES_V7X = 128
SUBLANES_V7X = 8
MXU_DIM_V7X = 256
VMEM_BYTES_V7X = 64 * 1024 * 1024
VMEM_LIMIT = VMEM_BYTES_V7X - 6 * 1024 * 1024


def _params(*sem):
    return pltpu.CompilerParams(dimension_semantics=sem, vmem_limit_bytes=VMEM_LIMIT)


def _pick(n, pref, mult=LANES_V7X):
    if n <= pref:
        return n
    best = None
    for d in range(mult, pref + 1, mult):
        if n % d == 0:
            best = d
    assert best is not None, (n, pref, mult)
    return best


def _gelu(x):
    c = math.sqrt(2.0 / math.pi)
    return x * (0.5 * (1.0 + jnp.tanh(c * (x + 0.044715 * (x * x * x)))))


def _sigmoid(x):
    return 1.0 / (1.0 + jnp.exp(-x))


def _ada_kernel(c_ref, w_ref, b_ref, o_ref):
    c = c_ref[...]
    s = (c * _sigmoid(c)).astype(BF16)
    o_ref[...] = jnp.dot(s, w_ref[...].astype(BF16), preferred_element_type=F32) + b_ref[...]


def _ada_table(cond, ada_w, ada_b):
    n_layers, d, n6 = ada_w.shape
    tn = _pick(n6, 1024)
    return pl.pallas_call(
        _ada_kernel,
        grid=(n_layers, n6 // tn),
        in_specs=[
            pl.BlockSpec((SUBLANES_V7X, d), lambda l, j: (0, 0)),
            pl.BlockSpec((None, d, tn), lambda l, j: (l, 0, j)),
            pl.BlockSpec((None, 1, tn), lambda l, j: (l, 0, j)),
        ],
        out_specs=pl.BlockSpec((None, SUBLANES_V7X, tn), lambda l, j: (l, 0, j)),
        out_shape=jax.ShapeDtypeStruct((n_layers, SUBLANES_V7X, n6), F32),
        compiler_params=_params("parallel", "arbitrary"),
        name="ada_table",
    )(cond, ada_w, ada_b.reshape(n_layers, 1, n6))


class _Groups:
    def __init__(self, t_ctx, n_lat):
        self.t_ctx = t_ctx
        self.n_lat = n_lat

    def of_tile(self, i, tm):
        assert self.t_ctx % tm == 0 and self.n_lat % tm == 0
        n_ctx_tiles = self.t_ctx // tm
        per_lat = self.n_lat // tm
        return jnp.where(i < n_ctx_tiles, 0, 1 + (i - n_ctx_tiles) // per_lat)


def _mod_spec(groups, layer, which, tm, d, tile_axis, col_axis=None, tn=None):
    width = d if tn is None else tn

    def index_map(*ids):
        g = groups.of_tile(ids[tile_axis], tm)
        col = 0 if col_axis is None else ids[col_axis]
        return (layer, g, which, 0, col)

    return pl.BlockSpec((None, None, None, 1, width), index_map)


def _norm_mod_kernel(x_ref, g_ref, sh_ref, sc_ref, *rest, router):
    x = x_ref[...]
    y = x * lax.rsqrt(jnp.mean(x * x, axis=-1, keepdims=True) + EPS) * g_ref[...]
    h = (y * (1.0 + sc_ref[...]) + sh_ref[...]).astype(BF16)
    if router:
        rw_ref, o_ref, lg_ref = rest
        lg_ref[...] = jnp.dot(h, rw_ref[...], preferred_element_type=F32)
    else:
        (o_ref,) = rest
    o_ref[...] = h


def _norm_mod(x, gain, mod, groups, layer, which_shift, which_scale, router_w=None):
    t, d = x.shape
    tm = _pick(math.gcd(groups.t_ctx, groups.n_lat), 512, SUBLANES_V7X)
    in_specs = [
        pl.BlockSpec((tm, d), lambda i: (i, 0)),
        pl.BlockSpec((1, d), lambda i: (0, 0)),
        _mod_spec(groups, layer, which_shift, tm, d, 0),
        _mod_spec(groups, layer, which_scale, tm, d, 0),
    ]
    out_specs = pl.BlockSpec((tm, d), lambda i: (i, 0))
    out_shape = jax.ShapeDtypeStruct((t, d), BF16)
    args = [x, gain.reshape(1, d), mod, mod]
    if router_w is not None:
        e_pad = router_w.shape[1]
        in_specs.append(pl.BlockSpec((d, e_pad), lambda i: (0, 0)))
        out_specs = [out_specs, pl.BlockSpec((tm, e_pad), lambda i: (i, 0))]
        out_shape = [out_shape, jax.ShapeDtypeStruct((t, e_pad), F32)]
        args.append(router_w)
    return pl.pallas_call(
        functools.partial(_norm_mod_kernel, router=router_w is not None),
        grid=(t // tm,),
        in_specs=in_specs,
        out_specs=out_specs,
        out_shape=out_shape,
        compiler_params=_params("parallel"),
        name="norm_mod",
    )(*args)


def _mm_kernel(x_ref, w_ref, *rest, residual):
    if residual:
        res_ref, gate_ref, o_ref, wb_ref = rest
    else:
        o_ref, wb_ref = rest

    @pl.when(pl.program_id(1) == 0)
    def _():
        wb_ref[...] = w_ref[...].astype(BF16)

    acc = jnp.dot(x_ref[...], wb_ref[...], preferred_element_type=F32)
    if residual:
        o_ref[...] = res_ref[...] + gate_ref[...] * acc
    else:
        o_ref[...] = acc


def _matmul(x, w, layer, *, res=None, mod=None, groups=None, which_gate=None, tm_pref=1024, tn_pref=1024):
    t, k = x.shape
    n = w.shape[2]
    tm = _pick(math.gcd(groups.t_ctx, groups.n_lat), tm_pref, SUBLANES_V7X)
    tn = _pick(n, tn_pref)
    residual = res is not None
    in_specs = [
        pl.BlockSpec((tm, k), lambda j, i: (i, 0)),
        pl.BlockSpec((None, k, tn), lambda j, i: (layer, 0, j)),
    ]
    args = [x, w]
    if residual:
        in_specs += [
            pl.BlockSpec((tm, tn), lambda j, i: (i, j)),
            _mod_spec(groups, layer, which_gate, tm, n, 1, 0, tn),
        ]
        args += [res, mod]
    return pl.pallas_call(
        functools.partial(_mm_kernel, residual=residual),
        grid=(n // tn, t // tm),
        in_specs=in_specs,
        out_specs=pl.BlockSpec((tm, tn), lambda j, i: (i, j)),
        out_shape=jax.ShapeDtypeStruct((t, n), F32),
        scratch_shapes=[pltpu.VMEM((k, tn), BF16)],
        compiler_params=_params("parallel", "arbitrary"),
        name="matmul_res" if residual else "matmul",
    )(*args)


def _rglru_kernel(xr_ref, gr_ref, cw_ref, cb_ref, wa_ref, wx_ref, ba_ref, bx_ref, lam_ref, h0_ref,
                  y_ref, st_ref, af_ref, bf_ref, ab_ref, bb_ref, *, n_seq, seq_len):
    rows = n_seq * seq_len
    x = xr_ref[...]
    t = lax.rem(lax.broadcasted_iota(jnp.int32, x.shape, 0), seq_len)
    cw = cw_ref[...]
    xm1 = jnp.where(t >= 1, pltpu.roll(x, 1, 0), 0.0)
    xp1 = jnp.where(t < seq_len - 1, pltpu.roll(x, rows - 1, 0), 0.0)
    xp2 = jnp.where(t < seq_len - 2, pltpu.roll(x, rows - 2, 0), 0.0)
    xc = xm1 * cw[0:1] + x * cw[1:2] + xp1 * cw[2:3] + xp2 * cw[3:4] + cb_ref[...]
    xb = xc.astype(BF16)
    for d, (a_ref, b_ref) in enumerate(((af_ref, bf_ref), (ab_ref, bb_ref))):
        r = _sigmoid(jnp.dot(xb, wa_ref[d], preferred_element_type=F32) + ba_ref[d:d + 1, :])
        gi = _sigmoid(jnp.dot(xb, wx_ref[d], preferred_element_type=F32) + bx_ref[d:d + 1, :])
        lam = lam_ref[d:d + 1, :]
        log_sig = jnp.minimum(lam, 0.0) - jnp.log1p(jnp.exp(-jnp.abs(lam)))
        a = jnp.exp(RG_C * r * log_sig)
        a_ref[...] = a
        b_ref[...] = jnp.sqrt(1.0 - a * a) * (gi * xc)

    def step(i, carry):
        hf, hb = carry
        new_f, new_b = [], []
        for s in range(n_seq):
            rf = s * seq_len + i
            h = af_ref[pl.ds(rf, 1), :] * hf[s] + bf_ref[pl.ds(rf, 1), :]
            bf_ref[pl.ds(rf, 1), :] = h
            new_f.append(h)
            rb = s * seq_len + (seq_len - 1 - i)
            h = ab_ref[pl.ds(rb, 1), :] * hb[s] + bb_ref[pl.ds(rb, 1), :]
            bb_ref[pl.ds(rb, 1), :] = h
            new_b.append(h)
        return tuple(new_f), tuple(new_b)

    init = (tuple(h0_ref[s, 0:1, :] for s in range(n_seq)), tuple(h0_ref[s, 1:2, :] for s in range(n_seq)))
    hf, hb = lax.fori_loop(0, seq_len, step, init)
    for s in range(n_seq):
        st_ref[s, 0:1, :] = hf[s]
        st_ref[s, 1:2, :] = hb[s]
    y_ref[...] = ((bf_ref[...] + bb_ref[...]) * _gelu(gr_ref[...])).astype(BF16)


def _rglru(proj, row0, n_seqs, seq_len, mix_w, conv_w, conv_b, wa, wx, ba, bx, lam, h0):
    c = wa.shape[-1]
    n_ct = mix_w // c
    n_seq = max(1, min(n_seqs, 2048 // seq_len))
    while n_seqs % n_seq or row0 % (n_seq * seq_len):
        n_seq -= 1
    rows = n_seq * seq_len
    rb0 = row0 // rows
    gr_col0 = mix_w // c
    vec = lambda: pl.BlockSpec((1, c), lambda s, j: (0, j))
    vec2 = lambda: pl.BlockSpec((2, c), lambda s, j: (0, j))
    return pl.pallas_call(
        functools.partial(_rglru_kernel, n_seq=n_seq, seq_len=seq_len),
        grid=(n_seqs // n_seq, n_ct),
        in_specs=[
            pl.BlockSpec((rows, c), lambda s, j: (rb0 + s, j)),
            pl.BlockSpec((rows, c), lambda s, j: (rb0 + s, gr_col0 + j)),
            pl.BlockSpec((CONV_W, c), lambda s, j: (0, j)),
            vec(),
            pl.BlockSpec((2, None, c, c), lambda s, j: (0, j, 0, 0)),
            pl.BlockSpec((2, None, c, c), lambda s, j: (0, j, 0, 0)),
            vec2(), vec2(), vec2(),
            pl.BlockSpec((n_seq, 2, c), lambda s, j: (s, 0, j)),
        ],
        out_specs=[
            pl.BlockSpec((rows, c), lambda s, j: (s, j)),
            pl.BlockSpec((n_seq, 2, c), lambda s, j: (s, 0, j)),
        ],
        out_shape=[
            jax.ShapeDtypeStruct((n_seqs * seq_len, mix_w), BF16),
            jax.ShapeDtypeStruct((n_seqs, 2, mix_w), F32),
        ],
        scratch_shapes=[pltpu.VMEM((rows, c), F32) for _ in range(4)],
        compiler_params=_params("parallel", "arbitrary"),
        name="rglru",
    )(proj, proj, conv_w, conv_b.reshape(1, mix_w), wa, wx, ba, bx, lam, h0)


def _subhead_rms(x, ones, gain, inv_dh):
    sq = x * x
    hi = sq.astype(BF16)
    lo = (sq - hi.astype(F32)).astype(BF16)
    ss = jnp.dot(hi, ones, preferred_element_type=F32) + jnp.dot(lo, ones, preferred_element_type=F32)
    return x * lax.rsqrt(ss * inv_dh + EPS) * gain


def _rope(x, cos, sin_signed, half):
    width = x.shape[1]
    lane = lax.rem(lax.broadcasted_iota(jnp.int32, x.shape, 1), 2 * half)
    partner = jnp.where(lane < half, pltpu.roll(x, width - half, 1), pltpu.roll(x, half, 1))
    return x * cos + partner * sin_signed


def _qk_prep_kernel(q_ref, k_ref, gq_ref, gk_ref, ones_ref, *rest, rope, cache, dh):
    rest = list(rest)
    if rope:
        cos_ref, sin_ref = rest[:2]
        rest = rest[2:]
    qn_ref, kn_ref = rest[:2]
    ones = ones_ref[...]
    q = _subhead_rms(q_ref[...], ones, gq_ref[...], 1.0 / dh)
    k = _subhead_rms(k_ref[...], ones, gk_ref[...], 1.0 / dh)
    if cache:
        rest[2][...] = k
    if rope:
        q = _rope(q, cos_ref[...], sin_ref[...], dh // 4)
        k = _rope(k, cos_ref[...], sin_ref[...], dh // 4)
    qn_ref[...] = q.astype(BF16)
    kn_ref[...] = k.astype(BF16)


def _qk_prep(proj, row0, n_rows, mix_w, dh, gq, gk, ones, rope_tabs, seq_len, cache):
    c = ones.shape[0]
    n_ct = mix_w // c
    tm = _pick(seq_len, 512, SUBLANES_V7X)
    assert row0 % tm == 0
    rb0 = row0 // tm
    q_col0, k_col0 = 2 * n_ct, 3 * n_ct
    rope = rope_tabs is not None
    in_specs = [
        pl.BlockSpec((tm, c), lambda i, j: (rb0 + i, q_col0 + j)),
        pl.BlockSpec((tm, c), lambda i, j: (rb0 + i, k_col0 + j)),
        pl.BlockSpec((1, c), lambda i, j: (0, 0)),
        pl.BlockSpec((1, c), lambda i, j: (0, 0)),
        pl.BlockSpec((c, c), lambda i, j: (0, 0)),
    ]
    args = [proj, proj, gq, gk, ones]
    if rope:
        per_seq = seq_len // tm
        in_specs += [pl.BlockSpec((tm, c), lambda i, j: (i % per_seq, 0))] * 2
        args += list(rope_tabs)
    out_specs = [pl.BlockSpec((tm, c), lambda i, j: (i, j))] * 2
    out_shape = [jax.ShapeDtypeStruct((n_rows, mix_w), BF16)] * 2
    if cache:
        out_specs.append(pl.BlockSpec((tm, c), lambda i, j: (i, j)))
        out_shape.append(jax.ShapeDtypeStruct((n_rows, mix_w), F32))
    return pl.pallas_call(
        functools.partial(_qk_prep_kernel, rope=rope, cache=cache, dh=dh),
        grid=(n_rows // tm, n_ct),
        in_specs=in_specs,
        out_specs=out_specs,
        out_shape=out_shape,
        compiler_params=_params("parallel", "parallel"),
        name="qk_prep",
    )(*args)


def _attn_kernel(lq_ref, lk_ref, g_ref, q_ref, *rest, n_src, lam_init, scale, dh, copy_v):
    kv = rest[:2 * n_src]
    o_ref = rest[2 * n_src]
    q = q_ref[...]
    lane = lax.broadcasted_iota(jnp.int32, q.shape, 1)
    lq, lk = lq_ref[...], lk_ref[...]
    lam = (jnp.exp(jnp.sum(lq[0:1] * lk[0:1], axis=-1, keepdims=True))
           - jnp.exp(jnp.sum(lq[1:2] * lk[1:2], axis=-1, keepdims=True)) + lam_init)
    ks = [kv[2 * i][...].astype(BF16) for i in range(n_src)]
    vs_f32 = [kv[2 * i + 1][...] for i in range(n_src)]
    vs = [v.astype(BF16) for v in vs_f32]
    probs = []
    for m in range(2):
        qm = jnp.where((lane >= m * dh) & (lane < (m + 1) * dh), q, jnp.zeros_like(q))
        ss = [lax.dot_general(qm, k, (((1,), (1,)), ((), ())), preferred_element_type=F32) * scale for k in ks]
        mx = functools.reduce(jnp.maximum, [jnp.max(s, axis=-1, keepdims=True) for s in ss])
        es = [jnp.exp(s - mx) for s in ss]
        den = functools.reduce(lambda a, b: a + b, [jnp.sum(e, axis=-1, keepdims=True) for e in es])
        inv = 1.0 / den
        probs.append([e * inv for e in es])
    o = None
    for i in range(n_src):
        w = (probs[0][i] - lam * probs[1][i]).astype(BF16)
        part = jnp.dot(w, vs[i], preferred_element_type=F32)
        o = part if o is None else o + part
    y = o * lax.rsqrt(jnp.mean(o * o, axis=-1, keepdims=True) + EPS) * g_ref[...]
    o_ref[...] = (y * (1.0 - lam_init)).astype(BF16)
    if copy_v:
        rest[2 * n_src + 1][...] = vs_f32[n_src - 1]


def _attention(qn, kn, proj, row0, n_seqs, seq_len, n_heads, dv, mix_w, lq, lk, subln_g, lam_init,
               ctx_k=None, ctx_v=None, layer=None, copy_v=False):
    dh = dv // 2
    tq = _pick(seq_len, 256, SUBLANES_V7X)
    n_qb = seq_len // tq
    assert row0 % seq_len == 0
    sb0 = row0 // seq_len
    v_col0 = 4 * n_heads
    small = lambda shape: pl.BlockSpec(shape, lambda b, h, i: (0, 0))
    in_specs = [small((2, dh)), small((2, dh)), small((1, dv)),
                pl.BlockSpec((tq, dv), lambda b, h, i: (b * n_qb + i, h))]
    args = [lq, lk, subln_g.reshape(1, dv), qn]
    n_src = 1
    if ctx_k is not None:
        past = ctx_k.shape[2]
        in_specs += [pl.BlockSpec((None, None, past, dv), lambda b, h, i: (b, layer, 0, h))] * 2
        args += [ctx_k, ctx_v]
        n_src = 2
    in_specs += [pl.BlockSpec((seq_len, dv), lambda b, h, i: (b, h)),
                 pl.BlockSpec((seq_len, dv), lambda b, h, i: (sb0 + b, v_col0 + h))]
    args += [kn, proj]
    out_specs = [pl.BlockSpec((tq, dv), lambda b, h, i: (b * n_qb + i, h))]
    out_shape = [jax.ShapeDtypeStruct((n_seqs * seq_len, mix_w), BF16)]
    if copy_v:
        assert n_qb == 1
        out_specs.append(pl.BlockSpec((seq_len, dv), lambda b, h, i: (b, h)))
        out_shape.append(jax.ShapeDtypeStruct((n_seqs * seq_len, mix_w), F32))
    return pl.pallas_call(
        functools.partial(_attn_kernel, n_src=n_src, lam_init=lam_init, scale=dh ** -0.5, dh=dh, copy_v=copy_v),
        grid=(n_seqs, n_heads, n_qb),
        in_specs=in_specs,
        out_specs=out_specs,
        out_shape=out_shape,
        compiler_params=_params("parallel", "parallel", "arbitrary"),
        name="diff_attn",
    )(*args)


def _gmlp_kernel(u_ref, v_ref, g_ref, ws_ref, bs_ref, o_ref, *, n_chunks, group_ch):
    width = u_ref.shape[1]
    per_tile = LANES_V7X // group_ch
    lane = lax.broadcasted_iota(jnp.int32, (CHUNK, LANES_V7X), 1)
    for c in range(n_chunks):
        rows = slice(c * CHUNK, (c + 1) * CHUNK)
        vg = _gelu(v_ref[rows, :])
        vn = (vg * lax.rsqrt(jnp.mean(vg * vg, axis=-1, keepdims=True) + EPS) * g_ref[...]).astype(BF16)
        for j in range(width // LANES_V7X):
            cols = slice(j * LANES_V7X, (j + 1) * LANES_V7X)
            vj = vn[:, cols]
            rhs = jnp.concatenate(
                [jnp.where((lane >= p * group_ch) & (lane < (p + 1) * group_ch), vj, jnp.zeros_like(vj))
                 for p in range(per_tile)], axis=0)
            s = jnp.dot(ws_ref[j], rhs, preferred_element_type=F32) + bs_ref[:, cols]
            o_ref[rows, cols] = (_gelu(u_ref[rows, cols]) * s).astype(BF16)


def _gmlp(proj, mix_w, gain, ws_tiles, bs_lanes, group_ch):
    t = proj.shape[0]
    n_chunks = 2 if t % (2 * CHUNK) == 0 else 1
    tm = n_chunks * CHUNK
    n_tiles = mix_w // LANES_V7X
    return pl.pallas_call(
        functools.partial(_gmlp_kernel, n_chunks=n_chunks, group_ch=group_ch),
        grid=(t // tm,),
        in_specs=[
            pl.BlockSpec((tm, mix_w), lambda i: (i, 5)),
            pl.BlockSpec((tm, mix_w), lambda i: (i, 6)),
            pl.BlockSpec((1, mix_w), lambda i: (0, 0)),
            pl.BlockSpec((n_tiles, CHUNK, ws_tiles.shape[2]), lambda i: (0, 0, 0)),
            pl.BlockSpec((CHUNK, mix_w), lambda i: (0, 0)),
        ],
        out_specs=pl.BlockSpec((tm, mix_w), lambda i: (i, 0)),
        out_shape=jax.ShapeDtypeStruct((t, mix_w), BF16),
        compiler_params=_params("parallel"),
        name="gmlp",
    )(proj, proj, gain.reshape(1, mix_w), ws_tiles, bs_lanes)


def _merge_kernel(ya_ref, yb_ref, yc_ref, g0_ref, g1_ref, g2_ref, wb_ref, o_ref, wbb_ref):
    @pl.when(pl.program_id(1) == 0)
    def _():
        wbb_ref[...] = wb_ref[...].astype(BF16)

    acc = None
    for br, (y_ref, g_ref) in enumerate(((ya_ref, g0_ref), (yb_ref, g1_ref), (yc_ref, g2_ref))):
        term = _sigmoid(g_ref[...]) * jnp.dot(y_ref[...], wbb_ref[br], preferred_element_type=F32)
        acc = term if acc is None else acc + term
    o_ref[...] = acc.astype(BF16)


def _merge(ya, yb, yc, proj, w_branch, layer, mix_w, d):
    t = ya.shape[0]
    tm = _pick(t, 1024, SUBLANES_V7X)
    tn = _pick(math.gcd(d, mix_w), 512)
    gate_col0 = 7 * mix_w // tn
    per_branch = d // tn
    y_spec = pl.BlockSpec((tm, mix_w), lambda j, i: (i, 0))
    gate_spec = lambda br: pl.BlockSpec((tm, tn), lambda j, i: (i, gate_col0 + br * per_branch + j))
    return pl.pallas_call(
        _merge_kernel,
        grid=(d // tn, t // tm),
        in_specs=[y_spec, y_spec, y_spec, gate_spec(0), gate_spec(1), gate_spec(2),
                  pl.BlockSpec((None, 3, mix_w, tn), lambda j, i: (layer, 0, 0, j))],
        out_specs=pl.BlockSpec((tm, tn), lambda j, i: (i, j)),
        out_shape=jax.ShapeDtypeStruct((t, d), BF16),
        scratch_shapes=[pltpu.VMEM((3, mix_w, tn), BF16)],
        compiler_params=_params("parallel", "arbitrary"),
        name="branch_merge",
    )(ya, yb, yc, proj, proj, proj, w_branch)


def _swiglu_kernel(be_ref, bv_ref, x_ref, w1_ref, w3_ref, w2_ref, o_ref, *, sub):
    del be_ref
    rb = pl.program_id(0)
    valid = bv_ref[rb]

    @pl.when(pl.program_id(1) == 0)
    def _():
        o_ref[...] = jnp.zeros_like(o_ref)

    @pl.when(valid > 0)
    def _():
        w1 = w1_ref[...].astype(BF16)
        w3 = w3_ref[...].astype(BF16)
        w2 = w2_ref[...].astype(BF16)
        for s in range(x_ref.shape[0] // sub):
            @pl.when(valid > s * sub)
            def _():
                rows = slice(s * sub, (s + 1) * sub)
                x = x_ref[rows, :]
                a = jnp.dot(x, w1, preferred_element_type=F32)
                h = (a * _sigmoid(a)) * jnp.dot(x, w3, preferred_element_type=F32)
                o_ref[rows, :] += jnp.dot(h.astype(BF16), w2, preferred_element_type=F32)


def _swiglu(x, w1, w3, w2, block_e, block_valid, tm):
    r, d = x.shape
    f = w1.shape[2]
    tf = _pick(f, 256)
    nf = f // tf
    sub = _pick(tm, 256, SUBLANES_V7X)

    def fcol(rb, j, bv):
        return jnp.where(bv[rb] > 0, j, nf - 1)

    grid_spec = pltpu.PrefetchScalarGridSpec(
        num_scalar_prefetch=2,
        grid=(r // tm, nf),
        in_specs=[
            pl.BlockSpec((tm, d), lambda rb, j, be, bv: (rb, 0)),
            pl.BlockSpec((None, d, tf), lambda rb, j, be, bv: (be[rb], 0, fcol(rb, j, bv))),
            pl.BlockSpec((None, d, tf), lambda rb, j, be, bv: (be[rb], 0, fcol(rb, j, bv))),
            pl.BlockSpec((None, tf, d), lambda rb, j, be, bv: (be[rb], fcol(rb, j, bv), 0)),
        ],
        out_specs=pl.BlockSpec((tm, d), lambda rb, j, be, bv: (rb, 0)),
    )
    return pl.pallas_call(
        functools.partial(_swiglu_kernel, sub=sub),
        grid_spec=grid_spec,
        out_shape=jax.ShapeDtypeStruct((r, d), F32),
        compiler_params=_params("parallel", "arbitrary"),
        name="swiglu",
    )(block_e, block_valid, x, w1, w3, w2)


def _resid_kernel(x_ref, gate_ref, y_ref, o_ref):
    o_ref[...] = x_ref[...] + gate_ref[...] * y_ref[...]


def _resid2_kernel(x_ref, gate_ref, y0_ref, y1_ref, g_ref, o_ref):
    g = g_ref[...]
    o_ref[...] = x_ref[...] + gate_ref[...] * (y0_ref[...] * g[:, 0:1] + y1_ref[...] * g[:, 1:2])


def _residual(x, mod, groups, layer, which_gate, ys, slot_gate=None):
    t, d = x.shape
    tm = _pick(math.gcd(groups.t_ctx, groups.n_lat), 512, SUBLANES_V7X)
    row = pl.BlockSpec((tm, d), lambda i: (i, 0))
    in_specs = [row, _mod_spec(groups, layer, which_gate, tm, d, 0)] + [row] * len(ys)
    args = [x, mod] + list(ys)
    kern = _resid_kernel
    if slot_gate is not None:
        in_specs.append(pl.BlockSpec((tm, TOP_K), lambda i: (i, 0)))
        args.append(slot_gate)
        kern = _resid2_kernel
    return pl.pallas_call(
        kern,
        grid=(t // tm,),
        in_specs=in_specs,
        out_specs=row,
        out_shape=jax.ShapeDtypeStruct((t, d), F32),
        compiler_params=_params("parallel"),
        name="residual",
    )(*args)


def _route(logits, n_experts, tm):
    t = logits.shape[0]
    top_logit, top_idx = lax.top_k(logits[:, :n_experts], TOP_K)
    gate = jax.nn.softmax(top_logit, axis=-1)
    slot_e = top_idx.reshape(-1)
    onehot = (slot_e[:, None] == jnp.arange(n_experts)[None, :]).astype(jnp.int32)
    rank = jnp.take_along_axis(jnp.cumsum(onehot, axis=0) - onehot, slot_e[:, None], axis=1)[:, 0]
    counts = jnp.sum(onehot, axis=0)
    padded = (counts + tm - 1) // tm * tm
    pend = jnp.cumsum(padded)
    pstart = pend - padded
    dest = pstart[slot_e] + rank
    n_blocks = -(-(t * TOP_K) // tm) + n_experts
    src_tok = jnp.zeros((n_blocks * tm,), jnp.int32).at[dest].set(jnp.arange(t * TOP_K, dtype=jnp.int32) // TOP_K)
    block_row0 = jnp.arange(n_blocks, dtype=jnp.int32) * tm
    block_e = jnp.minimum(jnp.searchsorted(pend, block_row0, side="right"), n_experts - 1).astype(jnp.int32)
    block_valid = jnp.clip(counts[block_e] - (block_row0 - pstart[block_e]), 0, tm).astype(jnp.int32)
    return gate, dest.reshape(t, TOP_K), src_tok, block_e, block_valid


def _block_diag_tiles(w, tile):
    n_layers, two, heads, blk, _ = w.shape
    per = tile // blk
    n_tiles = heads // per
    w = w.reshape(n_layers, two, n_tiles, per, blk, blk)
    eye = jnp.eye(per, dtype=w.dtype)
    out = jnp.einsum("ldtpij,pq->ldtpiqj", w, eye)
    return out.reshape(n_layers, two, n_tiles, tile, tile).astype(BF16)


def _rope_tables(n_tokens, dh, width):
    axis_dim = dh // 2
    half = axis_dim // 2
    pos = jnp.arange(n_tokens)
    row = (pos // GRID_W).astype(F32)
    col = (pos % GRID_W).astype(F32)
    inv_freq = 1.0 / (ROPE_THETA ** (jnp.arange(0, axis_dim, 2, dtype=F32) / axis_dim))
    ang_r = row[:, None] * inv_freq[None, :]
    ang_c = col[:, None] * inv_freq[None, :]
    ang = jnp.concatenate([ang_r, ang_r, ang_c, ang_c], axis=1)
    sign = jnp.concatenate([-jnp.ones((half,), F32), jnp.ones((half,), F32)] * 2)
    reps = width // dh
    return jnp.tile(jnp.cos(ang), (1, reps)), jnp.tile(jnp.sin(ang) * sign[None, :], (1, reps))


def kernel(x_prompt, x_sample, cache_k, cache_v, state_rglru, c, c_ctx, ada_w, ada_b, norm1_g, norm2_g, w_in, conv_w, conv_b, rg_wa, rg_ba, rg_wx, rg_bx, rg_lambda, q_norm_g, k_norm_g, lambda_q, lambda_k, subln_g, gmlp_norm_g, gmlp_ws, gmlp_bs, w_branch, w_o, ffn_w1, ffn_w3, ffn_w2, router_w, moe_w1, moe_w3, moe_w2):
    batch, seq, d = x_prompt.shape
    dec_batch, dec_seq, _ = x_sample.shape
    depth = w_in.shape[0]
    mix_w = conv_w.shape[2]
    n_heads = cache_k.shape[3]
    dv = cache_v.shape[4]
    dh = dv // 2
    past = cache_k.shape[2]
    n_experts = router_w.shape[2]
    group_ch = mix_w // gmlp_ws.shape[1]
    t_ctx, t_lat = batch * seq, dec_batch * dec_seq
    t = t_ctx + t_lat
    assert 1 + dec_batch <= SUBLANES_V7X and dv == LANES_V7X and mix_w % MXU_DIM_V7X == 0
    assert seq % CHUNK == 0 and dec_seq % CHUNK == 0 and gmlp_ws.shape[2] == CHUNK
    groups = _Groups(t_ctx, dec_seq)

    cond = jnp.zeros((SUBLANES_V7X, d), F32).at[0].set(c_ctx).at[1:1 + dec_batch].set(c)
    mod = _ada_table(cond, ada_w, ada_b)[:, :1 + dec_batch].reshape(depth, 1 + dec_batch, 6, 1, d)

    tile = MXU_DIM_V7X
    wa_t = _block_diag_tiles(rg_wa, tile)
    wx_t = _block_diag_tiles(rg_wx, tile)
    ones = jnp.kron(jnp.eye(tile // dh, dtype=F32), jnp.ones((dh, dh), F32)).astype(BF16)
    gq_t = jnp.tile(q_norm_g, (1, tile // dh)).reshape(depth, 1, tile)
    gk_t = jnp.tile(k_norm_g, (1, tile // dh)).reshape(depth, 1, tile)
    rope_tabs = _rope_tables(dec_seq, dh, tile)
    per_tile = LANES_V7X // group_ch
    n_gt = mix_w // LANES_V7X
    ws_t = (gmlp_ws.reshape(depth, n_gt, per_tile, CHUNK, CHUNK).transpose(0, 1, 3, 2, 4)
            .reshape(depth, n_gt, CHUNK, per_tile * CHUNK).astype(BF16))
    bs_l = jnp.repeat(jnp.swapaxes(gmlp_bs, 1, 2), group_ch, axis=2)
    e_pad = LANES_V7X
    rw_pad = jnp.zeros((router_w.shape[0], d, e_pad), F32).at[:, :, :n_experts].set(router_w).astype(BF16)
    h0_ctx = jnp.zeros((batch, 2, mix_w), F32)
    ck = cache_k.reshape(dec_batch, depth, past, n_heads * dv)
    cv = cache_v.reshape(dec_batch, depth, past, n_heads * dv)

    x = jnp.concatenate([x_prompt.reshape(t_ctx, d), x_sample.reshape(t_lat, d)], axis=0)
    k_list, v_list, s_list = [], [], []
    tm_ffn = _pick(math.gcd(t_ctx, dec_seq), 1024, SUBLANES_V7X)
    for l in range(depth):
        lam_init = 0.8 - 0.6 * math.exp(-0.3 * l)

        hn = _norm_mod(x, norm1_g[l], mod, groups, l, 0, 1)
        proj = _matmul(hn, w_in, l, groups=groups)
        rg_args = (conv_w[l], conv_b[l], wa_t[l], wx_t[l], rg_ba[l], rg_bx[l], rg_lambda[l])
        ya_c, st_c = _rglru(proj, 0, batch, seq, mix_w, *rg_args, h0_ctx)
        ya_s, _ = _rglru(proj, t_ctx, dec_batch, dec_seq, mix_w, *rg_args, state_rglru[:, l])
        qn_c, kn_c, kc = _qk_prep(proj, 0, t_ctx, mix_w, dh, gq_t[l], gk_t[l], ones, None, seq, True)
        qn_s, kn_s = _qk_prep(proj, t_ctx, t_lat, mix_w, dh, gq_t[l], gk_t[l], ones, rope_tabs, dec_seq, False)
        at_args = (n_heads, dv, mix_w, lambda_q[l], lambda_k[l], subln_g[l], lam_init)
        yb_c, vc = _attention(qn_c, kn_c, proj, 0, batch, seq, *at_args, copy_v=True)
        (yb_s,) = _attention(qn_s, kn_s, proj, t_ctx, dec_batch, dec_seq, *at_args, ctx_k=ck, ctx_v=cv, layer=l)
        yc = _gmlp(proj, mix_w, gmlp_norm_g[l], ws_t[l], bs_l[l], group_ch)
        ya = jnp.concatenate([ya_c, ya_s], axis=0)
        yb = jnp.concatenate([yb_c, yb_s], axis=0)
        merged = _merge(ya, yb, yc, proj, w_branch, l, mix_w, d)
        x = _matmul(merged, w_o, l, res=x, mod=mod, groups=groups, which_gate=2, tn_pref=512)
        k_list.append(kc.reshape(batch, seq, n_heads, dv))
        v_list.append(vc.reshape(batch, seq, n_heads, dv))
        s_list.append(st_c)

        i = l // 2
        if l % 2 == 0:
            hn = _norm_mod(x, norm2_g[l], mod, groups, l, 3, 4)
            n_blocks = t // tm_ffn
            y = _swiglu(hn, ffn_w1, ffn_w3, ffn_w2, jnp.full((n_blocks,), i, jnp.int32),
                        jnp.full((n_blocks,), tm_ffn, jnp.int32), tm_ffn)
            x = _residual(x, mod, groups, l, 5, [y])
        else:
            hn, logits = _norm_mod(x, norm2_g[l], mod, groups, l, 3, 4, router_w=rw_pad[i])
            gate, dest, src_tok, block_e, block_valid = _route(logits, n_experts, tm_ffn)
            ybuf = _swiglu(hn[src_tok], moe_w1.reshape((-1,) + moe_w1.shape[2:]), moe_w3.reshape((-1,) + moe_w3.shape[2:]),
                           moe_w2.reshape((-1,) + moe_w2.shape[2:]), block_e + i * n_experts, block_valid, tm_ffn)
            x = _residual(x, mod, groups, l, 5, [ybuf[dest[:, 0]], ybuf[dest[:, 1]]], slot_gate=gate)

    y_prompt = x[:t_ctx].reshape(batch, seq, d)
    y_sample = x[t_ctx:].reshape(dec_batch, dec_seq, d)
    return (y_prompt, y_sample, jnp.stack(k_list, axis=1), jnp.stack(v_list, axis=1), jnp.stack(s_list, axis=1))
```
